```python
import jax, jax.numpy as jnp
from jax import lax
import numpy as np

D_MODEL = 2048
BATCH = 1
SEQ = 8192
DEPTH = 2
DEC_BATCH = 128
DEC_SEQ = 4
PAST_LEN = 2048
PAGE_SIZE = 128

RET_HEADS = 8
RET_DK = D_MODEL // 32
RET_DV = D_MODEL // 16
SB_HEADS = 8
SB_DH = D_MODEL // 16
RET_QK = RET_HEADS * RET_DK
RET_WIDTH = RET_HEADS * RET_DV
SB_WIDTH = SB_HEADS * SB_DH
MIX_WIDTH = RET_WIDTH + SB_WIDTH
IN_SIZES = (RET_QK, RET_QK, RET_WIDTH, RET_WIDTH, SB_WIDTH, SB_WIDTH, SB_WIDTH)
IN_WIDTH = sum(IN_SIZES)
RET_CHUNK = 128
SB_BLOCK = 128
SB_BIAS_INIT = -7.0
D_FF = 11 * D_MODEL // 4
N_EXPERTS = 8
TOP_K = 2
D_EXPERT = D_FF // 2
N_DENSE = (DEPTH + 1) // 2
N_MOE = DEPTH // 2
ROPE_BASE = 10000.0
EPS = 1e-6

kernel_name = 'hymba_style_retention_stickbreaking_decode_step'


def rmsnorm(x, g):
    xf = x.astype(jnp.float32)
    xf = xf * lax.rsqrt(jnp.mean(xf * xf, axis=-1, keepdims=True) + EPS)
    return (xf * g.astype(jnp.float32)).astype(x.dtype)


def rotary(x, pos):
    half = x.shape[-1] // 2
    inv = ROPE_BASE ** (-jnp.arange(half, dtype=jnp.float32) / half)
    ang = pos[:, None] * inv[None, :]
    c = jnp.cos(ang)[None, :, None, :]
    s = jnp.sin(ang)[None, :, None, :]
    x1, x2 = x[..., :half], x[..., half:]
    return jnp.concatenate([x1 * c - x2 * s, x1 * s + x2 * c], axis=-1)


def retention_log_decay():
    return jnp.log1p(-jnp.exp2(-5.0 - jnp.arange(RET_HEADS, dtype=jnp.float32)))


def retention_chunk(S, qc, kc, vc):
    C = qc.shape[1]
    lg = retention_log_decay()
    idx = jnp.arange(C, dtype=jnp.float32)
    diff = idx[:, None] - idx[None, :]
    dmat = jnp.where(diff >= 0, jnp.exp(lg[:, None, None] * jnp.maximum(diff, 0.0)), 0.0)
    inner = jnp.einsum('bnhd,bmhd->bhnm', qc, kc) * dmat
    o = jnp.einsum('bhnm,bmhe->bnhe', inner, vc)
    o = o + jnp.einsum('bnhd,bhde->bnhe', qc, S) * jnp.exp(lg[None, :] * (idx[:, None] + 1.0))[None, :, :, None]
    kdec = kc * jnp.exp(lg[None, :] * (C - 1.0 - idx[:, None]))[None, :, :, None]
    S_new = jnp.exp(lg * C)[None, :, None, None] * S + jnp.einsum('bmhd,bmhe->bhde', kdec, vc)
    return S_new, o


def retention_prompt(q, k, v):
    B, T = q.shape[:2]
    n = T // RET_CHUNK
    def to_chunks(a):
        return a.reshape(B, n, RET_CHUNK, a.shape[2], a.shape[3]).transpose(1, 0, 2, 3, 4)
    S0 = jnp.zeros((B, RET_HEADS, RET_DK, RET_DV), jnp.float32)
    def step(S, xs):
        qc, kc, vc = xs
        return retention_chunk(S, qc, kc, vc)
    S_fin, o = lax.scan(step, S0, (to_chunks(q), to_chunks(k), to_chunks(v)))
    o = o.transpose(1, 0, 2, 3, 4).reshape(B, T, RET_HEADS, RET_DV)
    return S_fin, o


def sb_attend(q, k, v, bias, q_pos0):
    z = jnp.einsum('bqhd,bkhd->bhqk', q, k) * (SB_DH ** -0.5) + bias.astype(jnp.float32)[None, :, None, None]
    qpos = q_pos0 + jnp.arange(q.shape[1])
    kpos = jnp.arange(k.shape[1])
    visible = kpos[None, :] < qpos[:, None]
    log_1m = jnp.where(visible, jax.nn.log_sigmoid(-z), 0.0)
    after = lax.cumsum(log_1m, axis=3, reverse=True) - log_1m
    w = jnp.where(visible, jnp.exp(jax.nn.log_sigmoid(z) + after), 0.0)
    return jnp.einsum('bhqk,bkhd->bqhd', w, v)


def sb_prompt(q, k, v, bias):
    B, T, H, d = q.shape
    nb = T // SB_BLOCK
    qb = q.reshape(B, nb, SB_BLOCK, H, d).transpose(1, 0, 2, 3, 4)
    o = lax.map(lambda a: sb_attend(a[1], k, v, bias, a[0] * SB_BLOCK), (jnp.arange(nb), qb))
    return o.transpose(1, 0, 2, 3, 4).reshape(B, T, H, d)


def mixer_inputs(h, w_in_l, pos):
    B, T = h.shape[:2]
    p = (h @ w_in_l).astype(jnp.float32)
    rq, rk, rv, rg, sq, sk, sv = jnp.split(p, list(np.cumsum(IN_SIZES)[:-1]), axis=-1)
    rq = rotary(rq.reshape(B, T, RET_HEADS, RET_DK), pos)
    rk = rotary(rk.reshape(B, T, RET_HEADS, RET_DK), pos) * (RET_DK ** -0.5)
    rv = rv.reshape(B, T, RET_HEADS, RET_DV)
    sq = sq.reshape(B, T, SB_HEADS, SB_DH)
    sk = sk.reshape(B, T, SB_HEADS, SB_DH)
    sv = sv.reshape(B, T, SB_HEADS, SB_DH)
    return rq, rk, rv, rg, sq, sk, sv


def mixer_output(ret_o, rg, sb_o, ret_norm_g_l, w_out_l, dtype):
    B, T = ret_o.shape[:2]
    ro = ret_o * lax.rsqrt(jnp.mean(ret_o * ret_o, axis=-1, keepdims=True) + EPS)
    ro = ro.reshape(B, T, RET_WIDTH) * ret_norm_g_l.astype(jnp.float32)
    ro = jax.nn.silu(rg) * ro
    cat = jnp.concatenate([ro, sb_o.reshape(B, T, SB_WIDTH)], axis=-1).astype(dtype)
    return cat @ w_out_l


def swiglu(h, wg, wu, wd):
    return (jax.nn.silu(h @ wg) * (h @ wu)) @ wd


def moe(h, w_router_l, wg, wu, wd):
    logits = (h @ w_router_l).astype(jnp.float32)
    probs = jax.nn.softmax(logits, axis=-1)
    top_p, top_i = lax.top_k(probs, TOP_K)
    top_p = top_p / jnp.sum(top_p, axis=-1, keepdims=True)
    gates = jnp.sum(jax.nn.one_hot(top_i, N_EXPERTS, dtype=jnp.float32) * top_p[..., None], axis=-2)
    y = jnp.zeros_like(h)
    for e in range(N_EXPERTS):
        y = y + gates[..., e:e + 1].astype(h.dtype) * swiglu(h, wg[e], wu[e], wd[e])
    return y


def setup_inputs(seed: int = 0) -> dict:
    key = jax.random.key(seed)
    ks = jax.random.split(key, 24)
    n_pages = PAST_LEN // PAGE_SIZE
    n_used = DEC_BATCH * n_pages
    n_pool = n_used + max(1, n_used // 4)
    f32 = jnp.float32
    def nrm(k, shape, scale):
        return jax.random.normal(k, shape, f32) * scale
    perm = jax.random.permutation(ks[0], n_pool)[:n_used]
    return {
        'x_prompt': nrm(ks[1], (BATCH, SEQ, D_MODEL), 1.0),
        'x_sample': nrm(ks[2], (DEC_BATCH, DEC_SEQ, D_MODEL), 1.0),
        'cache_sb_k': nrm(ks[3], (DEPTH, n_pool, PAGE_SIZE, SB_HEADS, SB_DH), 1.0),
        'cache_sb_v': nrm(ks[4], (DEPTH, n_pool, PAGE_SIZE, SB_HEADS, SB_DH), 1.0),
        'state_ret': nrm(ks[5], (DEPTH, DEC_BATCH, RET_HEADS, RET_DK, RET_DV), 0.5),
        'page_table': perm.reshape(DEC_BATCH, n_pages).astype(jnp.int32),
        'g_mix': 1.0 + nrm(ks[6], (DEPTH, D_MODEL), 0.01),
        'w_in': nrm(ks[7], (DEPTH, D_MODEL, IN_WIDTH), D_MODEL ** -0.5),
        'ret_norm_g': 1.0 + nrm(ks[8], (DEPTH, RET_WIDTH), 0.01),
        'sb_bias': SB_BIAS_INIT + nrm(ks[19], (DEPTH, SB_HEADS), 0.1),
        'w_out': nrm(ks[9], (DEPTH, MIX_WIDTH, D_MODEL), MIX_WIDTH ** -0.5),
        'g_ffn': 1.0 + nrm(ks[10], (DEPTH, D_MODEL), 0.01),
        'w_ff_gate': nrm(ks[11], (N_DENSE, D_MODEL, D_FF), D_MODEL ** -0.5),
        'w_ff_up': nrm(ks[12], (N_DENSE, D_MODEL, D_FF), D_MODEL ** -0.5),
        'w_ff_down': nrm(ks[13], (N_DENSE, D_FF, D_MODEL), D_FF ** -0.5),
        'w_router': nrm(ks[14], (N_MOE, D_MODEL, N_EXPERTS), D_MODEL ** -0.5),
        'w_exp_gate': nrm(ks[15], (N_MOE, N_EXPERTS, D_MODEL, D_EXPERT), D_MODEL ** -0.5),
        'w_exp_up': nrm(ks[16], (N_MOE, N_EXPERTS, D_MODEL, D_EXPERT), D_MODEL ** -0.5),
        'w_exp_down': nrm(ks[17], (N_MOE, N_EXPERTS, D_EXPERT, D_MODEL), D_EXPERT ** -0.5),
        'g_final': 1.0 + nrm(ks[18], (D_MODEL,), 0.01),
    }


def reference(x_prompt, x_sample, cache_sb_k, cache_sb_v, state_ret, page_table,
              g_mix, w_in, ret_norm_g, sb_bias, w_out, g_ffn,
              w_ff_gate, w_ff_up, w_ff_down,
              w_router, w_exp_gate, w_exp_up, w_exp_down, g_final):
    xp, xs = x_prompt, x_sample
    dt = x_prompt.dtype
    pos_p = jnp.arange(SEQ, dtype=jnp.float32)
    pos_s = PAST_LEN + jnp.arange(DEC_SEQ, dtype=jnp.float32)
    n_past = page_table.shape[1] * PAGE_SIZE
    pk_list, pv_list, ps_list, sk_list, sv_list, ss_list = [], [], [], [], [], []
    for l in range(DEPTH):
        hp = rmsnorm(xp, g_mix[l])
        rq, rk, rv, rg, sq, sk, sv = mixer_inputs(hp, w_in[l], pos_p)
        S_p, ret_o = retention_prompt(rq, rk, rv)
        sb_o = sb_prompt(sq, sk, sv, sb_bias[l])
        xp = xp + mixer_output(ret_o, rg, sb_o, ret_norm_g[l], w_out[l], dt)
        pk_list.append(sk.astype(cache_sb_k.dtype))
        pv_list.append(sv.astype(cache_sb_v.dtype))
        ps_list.append(S_p.astype(state_ret.dtype))
        hs = rmsnorm(xs, g_mix[l])
        rq, rk, rv, rg, sq, sk, sv = mixer_inputs(hs, w_in[l], pos_s)
        S_s, ret_o = retention_chunk(state_ret[l].astype(jnp.float32), rq, rk, rv)
        past_k = cache_sb_k[l][page_table].reshape(DEC_BATCH, n_past, SB_HEADS, SB_DH).astype(jnp.float32)
        past_v = cache_sb_v[l][page_table].reshape(DEC_BATCH, n_past, SB_HEADS, SB_DH).astype(jnp.float32)
        k_all = jnp.concatenate([past_k, sk], axis=1)
        v_all = jnp.concatenate([past_v, sv], axis=1)
        sb_o = sb_attend(sq, k_all, v_all, sb_bias[l], PAST_LEN)
        xs = xs + mixer_output(ret_o, rg, sb_o, ret_norm_g[l], w_out[l], dt)
        sk_list.append(sk.astype(cache_sb_k.dtype))
        sv_list.append(sv.astype(cache_sb_v.dtype))
        ss_list.append(S_s.astype(state_ret.dtype))
        hp = rmsnorm(xp, g_ffn[l])
        hs = rmsnorm(xs, g_ffn[l])
        i = l // 2
        if l % 2 == 0:
            xp = xp + swiglu(hp, w_ff_gate[i], w_ff_up[i], w_ff_down[i])
            xs = xs + swiglu(hs, w_ff_gate[i], w_ff_up[i], w_ff_down[i])
        else:
            xp = xp + moe(hp, w_router[i], w_exp_gate[i], w_exp_up[i], w_exp_down[i])
            xs = xs + moe(hs, w_router[i], w_exp_gate[i], w_exp_up[i], w_exp_down[i])
    y_prompt = rmsnorm(xp, g_final)
    y_sample = rmsnorm(xs, g_final)
    return (y_prompt, y_sample, jnp.stack(pk_list), jnp.stack(pv_list), jnp.stack(ps_list),
            jnp.stack(sk_list), jnp.stack(sv_list), jnp.stack(ss_list))
```

```python
import functools

import jax
import jax.numpy as jnp
from jax import lax
from jax.experimental import pallas as pl
from jax.experimental.pallas import tpu as pltpu

F32 = jnp.float32
BF16 = jnp.bfloat16

D_MODEL = 2048
SEQ = 8192
DEPTH = 2
DEC_BATCH = 128
DEC_SEQ = 4
PAST_LEN = 2048
PAGE_SIZE = 128
RET_HEADS = 8
RET_DK = 64
RET_DV = 128
SB_HEADS = 8
SB_DH = 128
RET_QK = RET_HEADS * RET_DK
RET_WIDTH = RET_HEADS * RET_DV
SB_WIDTH = SB_HEADS * SB_DH
RET_CHUNK = 128
N_EXPERTS = 8
ROPE_BASE = 10000.0
EPS = 1e-6

OFF_RV = 2 * RET_QK
OFF_RG = OFF_RV + RET_WIDTH
OFF_SQ = OFF_RG + RET_WIDTH
OFF_SK = OFF_SQ + SB_WIDTH
OFF_SV = OFF_SK + SB_WIDTH

LANE = 128
SUBLANE = 8
VMEM_LIMIT = 52 * 1024 * 1024
SB_TK = LANE
ROW_PAD = SUBLANE


def _cparams(sem):
    return pltpu.CompilerParams(dimension_semantics=sem, vmem_limit_bytes=VMEM_LIMIT)


def _rmsnorm_body(x_ref, g_ref, o_ref):
    x = x_ref[...]
    ms = jnp.mean(x * x, axis=-1, keepdims=True)
    o_ref[...] = (x * lax.rsqrt(ms + EPS) * g_ref[...]).astype(o_ref.dtype)


def rmsnorm(x, g, out_dtype, tm=512):
    m, d = x.shape
    return pl.pallas_call(
        _rmsnorm_body,
        grid=(m // tm,),
        in_specs=[pl.BlockSpec((tm, d), lambda i: (i, 0)), pl.BlockSpec((1, d), lambda i: (0, 0))],
        out_specs=pl.BlockSpec((tm, d), lambda i: (i, 0)),
        out_shape=jax.ShapeDtypeStruct((m, d), out_dtype),
        compiler_params=_cparams(("arbitrary",)),
        name="rmsnorm",
    )(x, g.reshape(1, d))


def _rmsnorm_router_body(x_ref, g_ref, wr_ref, o_ref, gate_ref):
    x = x_ref[...]
    ms = jnp.mean(x * x, axis=-1, keepdims=True)
    h = x * lax.rsqrt(ms + EPS) * g_ref[...]
    o_ref[...] = h.astype(o_ref.dtype)
    logits = jnp.dot(h, wr_ref[...], preferred_element_type=F32, precision=lax.Precision.HIGHEST)
    lane = lax.broadcasted_iota(jnp.int32, logits.shape, 1)
    valid = lane < N_EXPERTS
    logits = jnp.where(valid, logits, -jnp.inf)
    mx = jnp.max(logits, axis=-1, keepdims=True)
    e = jnp.exp(logits - mx)
    probs = e / jnp.sum(e, axis=-1, keepdims=True)
    p1 = jnp.max(probs, axis=-1, keepdims=True)
    i1 = jnp.min(jnp.where(probs == p1, lane, LANE), axis=-1, keepdims=True)
    rest = jnp.where((lane == i1) | ~valid, -1.0, probs)
    p2 = jnp.max(rest, axis=-1, keepdims=True)
    i2 = jnp.min(jnp.where(rest == p2, lane, LANE), axis=-1, keepdims=True)
    tot = p1 + p2
    gate_ref[...] = jnp.where(lane == i1, p1 / tot, 0.0) + jnp.where(lane == i2, p2 / tot, 0.0)


def rmsnorm_router(x, g, w_router, tm=256):
    m, d = x.shape
    wr = jnp.zeros((d, LANE), F32).at[:, :N_EXPERTS].set(w_router)
    return pl.pallas_call(
        _rmsnorm_router_body,
        grid=(m // tm,),
        in_specs=[pl.BlockSpec((tm, d), lambda i: (i, 0)), pl.BlockSpec((1, d), lambda i: (0, 0)),
                  pl.BlockSpec((d, LANE), lambda i: (0, 0))],
        out_specs=[pl.BlockSpec((tm, d), lambda i: (i, 0)), pl.BlockSpec((tm, LANE), lambda i: (i, 0))],
        out_shape=[jax.ShapeDtypeStruct((m, d), BF16), jax.ShapeDtypeStruct((m, LANE), F32)],
        compiler_params=_cparams(("arbitrary",)),
        name="rmsnorm_router",
    )(x, g.reshape(1, d), wr)


def _mm_body(*refs, nk, has_res):
    if has_res:
        x_ref, w_ref, r_ref, o_ref = refs[:4]
    else:
        x_ref, w_ref, o_ref = refs[:3]
    part = jnp.dot(x_ref[...], w_ref[...], preferred_element_type=F32)
    if nk == 1:
        if has_res:
            part = part + r_ref[...]
        o_ref[...] = part.astype(o_ref.dtype)
        return
    acc_ref = refs[-1]
    k = pl.program_id(2)

    @pl.when(k == 0)
    def _():
        acc_ref[...] = part

    @pl.when(k > 0)
    def _():
        acc_ref[...] += part

    @pl.when(k == nk - 1)
    def _():
        r = acc_ref[...]
        if has_res:
            r = r + r_ref[...]
        o_ref[...] = r.astype(o_ref.dtype)


def matmul(x, w, residual=None, out_dtype=F32, tm=1024, tn=1024, tk=None):
    m, kd = x.shape
    n = w.shape[1]
    tm = min(tm, m)
    tn = min(tn, n)
    tk = kd if tk is None else tk
    nk = kd // tk
    has_res = residual is not None
    in_specs = [pl.BlockSpec((tm, tk), lambda j, i, k: (i, k)), pl.BlockSpec((tk, tn), lambda j, i, k: (k, j))]
    args = [x, w]
    if has_res:
        in_specs.append(pl.BlockSpec((tm, tn), lambda j, i, k: (i, j)))
        args.append(residual)
    return pl.pallas_call(
        functools.partial(_mm_body, nk=nk, has_res=has_res),
        grid=(n // tn, m // tm, nk),
        in_specs=in_specs,
        out_specs=pl.BlockSpec((tm, tn), lambda j, i, k: (i, j)),
        out_shape=jax.ShapeDtypeStruct((m, n), out_dtype),
        scratch_shapes=[pltpu.VMEM((tm, tn), F32)] if nk > 1 else [],
        compiler_params=_cparams(("arbitrary", "arbitrary", "arbitrary")),
        name="matmul",
    )(*args)


def _swiglu_body(*refs, has_gate):
    if has_gate:
        x_ref, wg_ref, wu_ref, gate_ref, o_ref = refs
    else:
        x_ref, wg_ref, wu_ref, o_ref = refs
    x = x_ref[...]
    g = jnp.dot(x, wg_ref[...], preferred_element_type=F32)
    u = jnp.dot(x, wu_ref[...], preferred_element_type=F32)
    a = g * jax.nn.sigmoid(g) * u
    if has_gate:
        e = pl.program_id(0)
        gates = gate_ref[...]
        lane = lax.broadcasted_iota(jnp.int32, gates.shape, 1)
        a = a * jnp.sum(jnp.where(lane == e, gates, 0.0), axis=-1, keepdims=True)
    o_ref[...] = a.astype(o_ref.dtype)


def swiglu_up(x, wg, wu, gates=None, tm=512, tf=1408):
    m, d = x.shape
    ne, _, f = wg.shape
    tm = min(tm, m)
    nf = f // tf
    has_gate = gates is not None
    in_specs = [pl.BlockSpec((tm, d), lambda e, j, i: (i, 0)),
                pl.BlockSpec((None, d, tf), lambda e, j, i: (e, 0, j)),
                pl.BlockSpec((None, d, tf), lambda e, j, i: (e, 0, j))]
    args = [x, wg, wu]
    if has_gate:
        in_specs.append(pl.BlockSpec((tm, LANE), lambda e, j, i: (i, 0)))
        args.append(gates)
    return pl.pallas_call(
        functools.partial(_swiglu_body, has_gate=has_gate),
        grid=(ne, nf, m // tm),
        in_specs=in_specs,
        out_specs=pl.BlockSpec((tm, tf), lambda e, j, i: (i, e * nf + j)),
        out_shape=jax.ShapeDtypeStruct((m, ne * f), BF16),
        compiler_params=_cparams(("arbitrary", "arbitrary", "arbitrary")),
        name="swiglu_up",
    )(*args)


def _rotary_body(x_ref, c_ref, s_ref, o_ref):
    x = x_ref[...]
    n = x.shape[1]
    lane = lax.broadcasted_iota(jnp.int32, x.shape, 1)
    first_half = (lane % RET_DK) < (RET_DK // 2)
    partner = jnp.where(first_half, pltpu.roll(x, n - RET_DK // 2, 1), pltpu.roll(x, RET_DK // 2, 1))
    o_ref[...] = x * c_ref[...] + partner * s_ref[...]


def rotary_qk(p, cos_t, sin_t, tm=512):
    m = p.shape[0]
    n = 2 * RET_QK
    return pl.pallas_call(
        _rotary_body,
        grid=(m // tm,),
        in_specs=[pl.BlockSpec((tm, n), lambda i: (i, 0)), pl.BlockSpec((tm, n), lambda i: (i, 0)),
                  pl.BlockSpec((tm, n), lambda i: (i, 0))],
        out_specs=pl.BlockSpec((tm, n), lambda i: (i, 0)),
        out_shape=jax.ShapeDtypeStruct((m, n), F32),
        compiler_params=_cparams(("arbitrary",)),
        name="rotary_qk",
    )(p, cos_t, sin_t)


def rotary_tables(pos):
    half = RET_DK // 2
    inv = ROPE_BASE ** (-jnp.arange(half, dtype=F32) / half)
    ang = pos[:, None] * inv[None, :]
    c, s = jnp.cos(ang), jnp.sin(ang)
    c_head = jnp.concatenate([c, c], axis=1)
    s_head = jnp.concatenate([-s, s], axis=1)
    c_q = jnp.tile(c_head, (1, RET_HEADS))
    s_q = jnp.tile(s_head, (1, RET_HEADS))
    k_scale = RET_DK ** -0.5
    return jnp.concatenate([c_q, c_q * k_scale], axis=1), jnp.concatenate([s_q, s_q * k_scale], axis=1)


def _ret_prompt_body(qk_ref, v_ref, dmat_ref, rowdec_ref, coldec_ref, sdec_ref, o_ref, s_ref):
    c = pl.program_id(0)

    @pl.when(c == 0)
    def _():
        s_ref[...] = jnp.zeros_like(s_ref)

    qk = qk_ref[...]
    q_all = qk[:, :RET_QK].astype(BF16)
    k_all = qk[:, RET_QK:]
    kdec_all = (k_all * coldec_ref[...]).astype(BF16)
    k_all = k_all.astype(BF16)
    v_all = v_ref[...].astype(BF16)
    for h in range(RET_HEADS):
        q = q_all[:, h * RET_DK:(h + 1) * RET_DK]
        k = k_all[:, h * RET_DK:(h + 1) * RET_DK]
        kdec = kdec_all[:, h * RET_DK:(h + 1) * RET_DK]
        v = v_all[:, h * RET_DV:(h + 1) * RET_DV]
        s = s_ref[h]
        inner = lax.dot_general(q, k, (((1,), (1,)), ((), ())), preferred_element_type=F32) * dmat_ref[h]
        o = jnp.dot(inner.astype(BF16), v, preferred_element_type=F32)
        o = o + jnp.dot(q, s.astype(BF16), preferred_element_type=F32) * rowdec_ref[h]
        o_ref[:, h * RET_DV:(h + 1) * RET_DV] = o
        s_ref[h] = sdec_ref[h] * s + lax.dot_general(kdec, v, (((0,), (0,)), ((), ())),
                                                     preferred_element_type=F32)


def _log_decay():
    return jnp.log1p(-jnp.exp2(-5.0 - jnp.arange(RET_HEADS, dtype=F32)))


def retention_prompt(qk, p):
    t = qk.shape[0]
    cs = RET_CHUNK
    lg = _log_decay()
    idx = jnp.arange(cs, dtype=F32)
    diff = idx[:, None] - idx[None, :]
    dmat = jnp.where(diff >= 0, jnp.exp(lg[:, None, None] * jnp.maximum(diff, 0.0)), 0.0)
    rowdec = jnp.broadcast_to(jnp.exp(lg[:, None] * (idx[None, :] + 1.0))[:, :, None], (RET_HEADS, cs, RET_DV))
    coldec = jnp.repeat(jnp.exp(lg[None, :] * (cs - 1.0 - idx[:, None])), RET_DK, axis=1)
    sdec = jnp.broadcast_to(jnp.exp(lg * cs)[:, None, None], (RET_HEADS, RET_DK, RET_DV))
    const3 = lambda shape: pl.BlockSpec(shape, lambda c: (0, 0, 0))
    return pl.pallas_call(
        _ret_prompt_body,
        grid=(t // cs,),
        in_specs=[pl.BlockSpec((cs, 2 * RET_QK), lambda c: (c, 0)),
                  pl.BlockSpec((cs, RET_WIDTH), lambda c: (c, OFF_RV // RET_WIDTH)),
                  const3((RET_HEADS, cs, cs)), const3((RET_HEADS, cs, RET_DV)),
                  pl.BlockSpec((cs, RET_QK), lambda c: (0, 0)), const3((RET_HEADS, RET_DK, RET_DV))],
        out_specs=[pl.BlockSpec((cs, RET_WIDTH), lambda c: (c, 0)), const3((RET_HEADS, RET_DK, RET_DV))],
        out_shape=[jax.ShapeDtypeStruct((t, RET_WIDTH), F32),
                   jax.ShapeDtypeStruct((RET_HEADS, RET_DK, RET_DV), F32)],
        compiler_params=_cparams(("arbitrary",)),
        name="retention_prompt",
    )(qk, p, dmat, rowdec, coldec, sdec)


def _ret_sample_body(qbd_ref, kbdt_ref, kdecbdt_ref, v_ref, s_ref, m_ref, rowdec_ref, sdec_ref, o_ref, sn_ref, *, bb):
    for i in range(bb):
        qbd = qbd_ref[i]
        v = v_ref[i]
        s = s_ref[i]
        inner = jnp.dot(qbd, kbdt_ref[i], preferred_element_type=F32) * m_ref[...]
        o = jnp.dot(inner.astype(BF16), v, preferred_element_type=F32)
        o = o + jnp.dot(qbd, s.astype(BF16), preferred_element_type=F32) * rowdec_ref[...]
        o_ref[i] = o
        sn_ref[i] = sdec_ref[...] * s + jnp.dot(kdecbdt_ref[i], v, preferred_element_type=F32)


def retention_sample(qk, rv, state, bb=4):
    b = state.shape[0]
    nt, nh = DEC_SEQ, RET_HEADS
    lg = _log_decay()
    q = qk[:, :RET_QK].reshape(b, nt, nh, RET_DK)
    k = qk[:, RET_QK:].reshape(b, nt, nh, RET_DK)
    idx = jnp.arange(nt, dtype=F32)
    kdec = k * jnp.exp(lg[None, :] * (nt - 1.0 - idx[:, None]))[None, :, :, None]
    eye = jnp.eye(nh, dtype=F32)
    qbd = jnp.einsum('bthd,hg->bhtgd', q, eye).reshape(b, nh * nt, nh * RET_DK).astype(BF16)
    kbdt = jnp.einsum('bthd,hg->bhdgt', k, eye).reshape(b, nh * RET_DK, nh * nt).astype(BF16)
    kdecbdt = jnp.einsum('bthd,hg->bhdgt', kdec, eye).reshape(b, nh * RET_DK, nh * nt).astype(BF16)
    v = rv.reshape(b, nt, nh, RET_DV).transpose(0, 2, 1, 3).reshape(b, nh * nt, RET_DV).astype(BF16)
    s = state.reshape(b, nh * RET_DK, RET_DV)
    diff = idx[:, None] - idx[None, :]
    dmat = jnp.where(diff >= 0, jnp.exp(lg[:, None, None] * jnp.maximum(diff, 0.0)), 0.0)
    mask = jnp.einsum('htu,hg->htgu', dmat, eye).reshape(nh * nt, nh * nt)
    rowdec = jnp.broadcast_to(jnp.exp(lg[:, None] * (idx[None, :] + 1.0)).reshape(nh * nt, 1), (nh * nt, RET_DV))
    sdec = jnp.broadcast_to(jnp.repeat(jnp.exp(lg * nt), RET_DK)[:, None], (nh * RET_DK, RET_DV))
    blk = lambda r, c: pl.BlockSpec((bb, r, c), lambda i: (i, 0, 0))
    cst = lambda r, c: pl.BlockSpec((r, c), lambda i: (0, 0))
    o, s_new = pl.pallas_call(
        functools.partial(_ret_sample_body, bb=bb),
        grid=(b // bb,),
        in_specs=[blk(nh * nt, nh * RET_DK), blk(nh * RET_DK, nh * nt), blk(nh * RET_DK, nh * nt),
                  blk(nh * nt, RET_DV), blk(nh * RET_DK, RET_DV),
                  cst(nh * nt, nh * nt), cst(nh * nt, RET_DV), cst(nh * RET_DK, RET_DV)],
        out_specs=[blk(nh * nt, RET_DV), blk(nh * RET_DK, RET_DV)],
        out_shape=[jax.ShapeDtypeStruct((b, nh * nt, RET_DV), F32),
                   jax.ShapeDtypeStruct((b, nh * RET_DK, RET_DV), F32)],
        compiler_params=_cparams(("arbitrary",)),
        name="retention_sample",
    )(qbd, kbdt, kdecbdt, v, s, mask, rowdec, sdec)
    o = o.reshape(b, nh, nt, RET_DV).transpose(0, 2, 1, 3).reshape(b * nt, RET_WIDTH)
    return o, s_new.reshape(b, nh, RET_DK, RET_DV)


def _ret_post_body(o_ref, g_ref, gain_ref, out_ref):
    gate = g_ref[...]
    gate = gate * jax.nn.sigmoid(gate)
    for h in range(RET_HEADS):
        sl = slice(h * RET_DV, (h + 1) * RET_DV)
        o = o_ref[:, sl]
        ro = o * lax.rsqrt(jnp.mean(o * o, axis=-1, keepdims=True) + EPS)
        out_ref[:, sl] = (gate[:, sl] * (ro * gain_ref[:, sl])).astype(out_ref.dtype)


def retention_post(o, p, gain, tm=512):
    m = o.shape[0]
    tm = min(tm, m)
    return pl.pallas_call(
        _ret_post_body,
        grid=(m // tm,),
        in_specs=[pl.BlockSpec((tm, RET_WIDTH), lambda i: (i, 0)),
                  pl.BlockSpec((tm, RET_WIDTH), lambda i: (i, OFF_RG // RET_WIDTH)),
                  pl.BlockSpec((1, RET_WIDTH), lambda i: (0, 0))],
        out_specs=pl.BlockSpec((tm, RET_WIDTH), lambda i: (i, 0)),
        out_shape=jax.ShapeDtypeStruct((m, RET_WIDTH), BF16),
        compiler_params=_cparams(("arbitrary",)),
        name="retention_post",
    )(o, p, gain.reshape(1, RET_WIDTH))


def _cumsum_matrix():
    kp = jnp.arange(2 * SB_TK)[:, None] % SB_TK
    col = jnp.arange(2 * SB_TK)[None, :]
    return jnp.where(col < SB_TK, kp > col, True).astype(BF16)


def _sb_block(q, kblk, vblk, uo, bias, carry, acc, mask):
    z = lax.dot_general(q, kblk, (((1,), (1,)), ((), ())), preferred_element_type=F32) + bias
    sp = jnp.maximum(z, 0.0) + jnp.log1p(jnp.exp(-jnp.abs(z)))
    lp = -sp
    if mask is not None:
        lp = jnp.where(mask, lp, 0.0)
    hi = lp.astype(BF16)
    lo = (lp - hi.astype(F32)).astype(BF16)
    cs = jnp.dot(jnp.concatenate([hi, lo], axis=1), uo, preferred_element_type=F32)
    w = jnp.exp((z - sp) + (carry + cs[:, :SB_TK]))
    if mask is not None:
        w = jnp.where(mask, w, 0.0)
    acc = acc + jnp.dot(w.astype(BF16), vblk, preferred_element_type=F32)
    return carry + cs[:, SB_TK:], acc


def _sb_prompt_body(bias_ref, q_ref, k_ref, v_ref, uo_ref, o_ref, kb_ref, vb_ref, *, tq):
    h = pl.program_id(0)
    i = pl.program_id(1)

    @pl.when(i == 0)
    def _():
        kb_ref[...] = k_ref[...].astype(BF16)
        vb_ref[...] = v_ref[...].astype(BF16)

    bias = bias_ref[h]
    q = (q_ref[...] * (SB_DH ** -0.5)).astype(BF16)
    uo = uo_ref[...]
    nd = tq // SB_TK
    row = lax.broadcasted_iota(jnp.int32, (tq, SB_TK), 0)
    col = lax.broadcasted_iota(jnp.int32, (tq, SB_TK), 1)
    carry = jnp.zeros((tq, SB_TK), F32)
    acc = jnp.zeros((tq, SB_DH), F32)
    for kk in reversed(range(nd)):
        start = pl.multiple_of(i * tq + kk * SB_TK, SB_TK)
        carry, acc = _sb_block(q, kb_ref[pl.ds(start, SB_TK), :], vb_ref[pl.ds(start, SB_TK), :], uo, bias,
                               carry, acc, (col + kk * SB_TK) < row)

    def body(t, ca):
        start = pl.multiple_of((i * nd - 1 - t) * SB_TK, SB_TK)
        return _sb_block(q, kb_ref[pl.ds(start, SB_TK), :], vb_ref[pl.ds(start, SB_TK), :], uo, bias,
                         ca[0], ca[1], None)

    carry, acc = lax.fori_loop(0, i * nd, body, (carry, acc))
    o_ref[...] = acc.astype(o_ref.dtype)


def sb_prompt(p, bias, tq=256):
    t = p.shape[0]
    cq, ck, cv = OFF_SQ // SB_DH, OFF_SK // SB_DH, OFF_SV // SB_DH
    grid_spec = pltpu.PrefetchScalarGridSpec(
        num_scalar_prefetch=1,
        grid=(SB_HEADS, t // tq),
        in_specs=[pl.BlockSpec((tq, SB_DH), lambda h, i, b: (i, cq + h)),
                  pl.BlockSpec((t, SB_DH), lambda h, i, b: (0, ck + h)),
                  pl.BlockSpec((t, SB_DH), lambda h, i, b: (0, cv + h)),
                  pl.BlockSpec((2 * SB_TK, 2 * SB_TK), lambda h, i, b: (0, 0))],
        out_specs=pl.BlockSpec((tq, SB_DH), lambda h, i, b: (i, h)),
        scratch_shapes=[pltpu.VMEM((t, SB_DH), BF16), pltpu.VMEM((t, SB_DH), BF16)],
    )
    return pl.pallas_call(
        functools.partial(_sb_prompt_body, tq=tq),
        grid_spec=grid_spec,
        out_shape=jax.ShapeDtypeStruct((t, SB_WIDTH), BF16),
        compiler_params=_cparams(("arbitrary", "arbitrary")),
        name="sb_prompt",
    )(bias, p, p, p, _cumsum_matrix())


def _sb_sample_body(pt_ref, q_ref, kn_ref, vn_ref, k_ref, v_ref, uo_ref, bias_ref, o_ref,
                    kpad_ref, vpad_ref, carry_ref, acc_ref, *, n_pages):
    b = pl.program_id(0)
    j = pl.program_id(1)
    nh = SB_HEADS
    rows = nh * ROW_PAD

    @pl.when((b == 0) & (j == 0))
    def _():
        kpad_ref[...] = jnp.zeros_like(kpad_ref)
        vpad_ref[...] = jnp.zeros_like(vpad_ref)

    def step(ksrc, vsrc, mask):
        uo = uo_ref[...]
        z = jnp.concatenate(
            [lax.dot_general(q_ref[h], ksrc[:, h, :].astype(BF16), (((1,), (1,)), ((), ())),
                             preferred_element_type=F32) for h in range(nh)], axis=0) + bias_ref[...]
        sp = jnp.maximum(z, 0.0) + jnp.log1p(jnp.exp(-jnp.abs(z)))
        lp = -sp
        if mask is not None:
            lp = jnp.where(mask, lp, 0.0)
        hi = lp.astype(BF16)
        lo = (lp - hi.astype(F32)).astype(BF16)
        cs = jnp.dot(jnp.concatenate([hi, lo], axis=1), uo, preferred_element_type=F32)
        w = jnp.exp((z - sp) + (carry_ref[...] + cs[:, :SB_TK]))
        if mask is not None:
            w = jnp.where(mask, w, 0.0)
        w = w.astype(BF16)
        for h in range(nh):
            acc_ref[h] += jnp.dot(w[h * ROW_PAD:(h + 1) * ROW_PAD], vsrc[:, h, :].astype(BF16),
                                  preferred_element_type=F32)
        carry_ref[...] += cs[:, SB_TK:]

    @pl.when(j == 0)
    def _():
        carry_ref[...] = jnp.zeros_like(carry_ref)
        acc_ref[...] = jnp.zeros_like(acc_ref)
        kpad_ref[0:DEC_SEQ] = kn_ref[...]
        vpad_ref[0:DEC_SEQ] = vn_ref[...]
        row = lax.broadcasted_iota(jnp.int32, (rows, SB_TK), 0)
        col = lax.broadcasted_iota(jnp.int32, (rows, SB_TK), 1)
        step(kpad_ref, vpad_ref, col < (row % ROW_PAD))

    @pl.when(j > 0)
    def _():
        step(k_ref, v_ref, None)

    @pl.when(j == n_pages)
    def _():
        o_ref[...] = acc_ref[...]


def sb_sample(layer, sq, sk, sv, cache_k, cache_v, page_table, bias):
    b, n_pages = page_table.shape
    nt, nh, dh = DEC_SEQ, SB_HEADS, SB_DH
    q = sq.reshape(b, nt, nh, dh).transpose(0, 2, 1, 3) * (dh ** -0.5)
    q = jnp.pad(q, ((0, 0), (0, 0), (0, ROW_PAD - nt), (0, 0))).astype(BF16)
    kn = sk.reshape(b, nt, nh, dh)
    vn = sv.reshape(b, nt, nh, dh)
    bias_rows = jnp.broadcast_to(jnp.repeat(bias.astype(F32), ROW_PAD)[:, None], (nh * ROW_PAD, SB_TK))

    def page_map(bi, j, pt):
        return (layer, pt[bi * n_pages + n_pages - jnp.maximum(j, 1)], 0, 0, 0)

    grid_spec = pltpu.PrefetchScalarGridSpec(
        num_scalar_prefetch=1,
        grid=(b, n_pages + 1),
        in_specs=[pl.BlockSpec((None, nh, ROW_PAD, dh), lambda bi, j, pt: (bi, 0, 0, 0)),
                  pl.BlockSpec((None, nt, nh, dh), lambda bi, j, pt: (bi, 0, 0, 0)),
                  pl.BlockSpec((None, nt, nh, dh), lambda bi, j, pt: (bi, 0, 0, 0)),
                  pl.BlockSpec((None, None, PAGE_SIZE, nh, dh), page_map),
                  pl.BlockSpec((None, None, PAGE_SIZE, nh, dh), page_map),
                  pl.BlockSpec((2 * SB_TK, 2 * SB_TK), lambda bi, j, pt: (0, 0)),
                  pl.BlockSpec((nh * ROW_PAD, SB_TK), lambda bi, j, pt: (0, 0))],
        out_specs=pl.BlockSpec((None, nh, ROW_PAD, dh), lambda bi, j, pt: (bi, 0, 0, 0)),
        scratch_shapes=[pltpu.VMEM((PAGE_SIZE, nh, dh), F32), pltpu.VMEM((PAGE_SIZE, nh, dh), F32),
                        pltpu.VMEM((nh * ROW_PAD, SB_TK), F32), pltpu.VMEM((nh, ROW_PAD, dh), F32)],
    )
    o = pl.pallas_call(
        functools.partial(_sb_sample_body, n_pages=n_pages),
        grid_spec=grid_spec,
        out_shape=jax.ShapeDtypeStruct((b, nh, ROW_PAD, dh), F32),
        compiler_params=_cparams(("arbitrary", "arbitrary")),
        name="sb_sample",
    )(page_table.reshape(-1), q, kn, vn, cache_k, cache_v, _cumsum_matrix(), bias_rows)
    return o[:, :, :nt, :].transpose(0, 2, 1, 3).reshape(b * nt, nh * dh)


def _mixer(x, l, is_prompt, wts, cos_t, sin_t, cache_k, cache_v, state_l, page_table):
    h = rmsnorm(x, wts['g_mix'][l], BF16)
    p = matmul(h, wts['w_in'][l])
    qk = rotary_qk(p, cos_t, sin_t)
    sk = p[:, OFF_SK:OFF_SV]
    sv = p[:, OFF_SV:]
    if is_prompt:
        ret_o, s_new = retention_prompt(qk, p)
        s_new = s_new[None]
        sb_o = sb_prompt(p, wts['sb_bias'][l])
    else:
        ret_o, s_new = retention_sample(qk, p[:, OFF_RV:OFF_RG], state_l)
        sb_o = sb_sample(l, p[:, OFF_SQ:OFF_SK], sk, sv, cache_k, cache_v, page_table,
                         wts['sb_bias'][l]).astype(BF16)
    ro = retention_post(ret_o, p, wts['ret_norm_g'][l])
    cat = jnp.concatenate([ro, sb_o], axis=1)
    x = matmul(cat, wts['w_out'][l], residual=x)
    return x, sk, sv, s_new


def _ffn(x, l, wts):
    i = l // 2
    if l % 2 == 0:
        h = rmsnorm(x, wts['g_ffn'][l], BF16)
        a = swiglu_up(h, wts['w_ff_gate'][i:i + 1], wts['w_ff_up'][i:i + 1])
        return matmul(a, wts['w_ff_down'][i], residual=x, tk=1408)
    h, gates = rmsnorm_router(x, wts['g_ffn'][l], wts['w_router'][i])
    a = swiglu_up(h, wts['w_exp_gate'][i], wts['w_exp_up'][i], gates=gates)
    wd = wts['w_exp_down'][i]
    return matmul(a, wd.reshape(wd.shape[0] * wd.shape[1], wd.shape[2]), residual=x, tk=1408)


def kernel(x_prompt, x_sample, cache_sb_k, cache_sb_v, state_ret, page_table, g_mix, w_in, ret_norm_g, sb_bias,
           w_out, g_ffn, w_ff_gate, w_ff_up, w_ff_down, w_router, w_exp_gate, w_exp_up, w_exp_down, g_final):
    wts = dict(g_mix=g_mix, w_in=w_in.astype(BF16), ret_norm_g=ret_norm_g, sb_bias=sb_bias,
               w_out=w_out.astype(BF16), g_ffn=g_ffn, w_ff_gate=w_ff_gate.astype(BF16),
               w_ff_up=w_ff_up.astype(BF16), w_ff_down=w_ff_down.astype(BF16), w_router=w_router,
               w_exp_gate=w_exp_gate.astype(BF16), w_exp_up=w_exp_up.astype(BF16),
               w_exp_down=w_exp_down.astype(BF16))
    n_s = DEC_BATCH * DEC_SEQ
    xp = x_prompt.reshape(SEQ, D_MODEL)
    xs = x_sample.reshape(n_s, D_MODEL)
    cos_p, sin_p = rotary_tables(jnp.arange(SEQ, dtype=F32))
    cos_s, sin_s = rotary_tables(jnp.tile(PAST_LEN + jnp.arange(DEC_SEQ, dtype=F32), DEC_BATCH))
    outs = [[] for _ in range(6)]
    for l in range(DEPTH):
        xp, pk, pv, ps = _mixer(xp, l, True, wts, cos_p, sin_p, None, None, None, None)
        xs, sk, sv, ss = _mixer(xs, l, False, wts, cos_s, sin_s, cache_sb_k, cache_sb_v, state_ret[l], page_table)
        for lst, val in zip(outs, (pk.reshape(1, SEQ, SB_HEADS, SB_DH), pv.reshape(1, SEQ, SB_HEADS, SB_DH), ps,
                                   sk.reshape(DEC_BATCH, DEC_SEQ, SB_HEADS, SB_DH),
                                   sv.reshape(DEC_BATCH, DEC_SEQ, SB_HEADS, SB_DH), ss)):
            lst.append(val)
        xp = _ffn(xp, l, wts)
        xs = _ffn(xs, l, wts)
    y_prompt = rmsnorm(xp, g_final, F32).reshape(1, SEQ, D_MODEL)
    y_sample = rmsnorm(xs, g_final, F32).reshape(DEC_BATCH, DEC_SEQ, D_MODEL)
    return (y_prompt, y_sample) + tuple(jnp.stack(lst) for lst in outs)
```

```python
import functools

import jax
import jax.numpy as jnp
from jax import lax
from jax.experimental import pallas as pl
from jax.experimental.pallas import tpu as pltpu

F32 = jnp.float32
BF16 = jnp.bfloat16

D_MODEL = 2048
SEQ = 8192
DEPTH = 2
DEC_BATCH = 128
DEC_SEQ = 4
PAST_LEN = 2048
PAGE_SIZE = 128
RET_HEADS = 8
RET_DK = 64
RET_DV = 128
SB_HEADS = 8
SB_DH = 128
RET_QK = RET_HEADS * RET_DK
RET_WIDTH = RET_HEADS * RET_DV
SB_WIDTH = SB_HEADS * SB_DH
RET_CHUNK = 128
N_EXPERTS = 8
ROPE_BASE = 10000.0
EPS = 1e-6

OFF_RV = 2 * RET_QK
OFF_RG = OFF_RV + RET_WIDTH
OFF_SQ = OFF_RG + RET_WIDTH
OFF_SK = OFF_SQ + SB_WIDTH
OFF_SV = OFF_SK + SB_WIDTH

LANE = 128
SUBLANE = 8
VMEM_LIMIT = 52 * 1024 * 1024
SB_TK = LANE
ROW_PAD = SUBLANE
LOG2E = 1.4426950408889634
SB_QSCALE = SB_DH ** -0.5 * LOG2E


def _cparams(sem):
    return pltpu.CompilerParams(dimension_semantics=sem, vmem_limit_bytes=VMEM_LIMIT)


def _rmsnorm_body(x_ref, g_ref, o_ref):
    x = x_ref[...]
    ms = jnp.mean(x * x, axis=-1, keepdims=True)
    o_ref[...] = (x * lax.rsqrt(ms + EPS) * g_ref[...]).astype(o_ref.dtype)


def rmsnorm(x, g, out_dtype, tm=512):
    m, d = x.shape
    return pl.pallas_call(
        _rmsnorm_body,
        grid=(m // tm,),
        in_specs=[pl.BlockSpec((tm, d), lambda i: (i, 0)), pl.BlockSpec((1, d), lambda i: (0, 0))],
        out_specs=pl.BlockSpec((tm, d), lambda i: (i, 0)),
        out_shape=jax.ShapeDtypeStruct((m, d), out_dtype),
        compiler_params=_cparams(("arbitrary",)),
        name="rmsnorm",
    )(x, g.reshape(1, d))


def _rmsnorm_router_body(x_ref, g_ref, wr_ref, o_ref, gate_ref):
    x = x_ref[...]
    ms = jnp.mean(x * x, axis=-1, keepdims=True)
    h = x * lax.rsqrt(ms + EPS) * g_ref[...]
    o_ref[...] = h.astype(o_ref.dtype)
    logits = jnp.dot(h, wr_ref[...], preferred_element_type=F32, precision=lax.Precision.HIGHEST)
    lane = lax.broadcasted_iota(jnp.int32, logits.shape, 1)
    valid = lane < N_EXPERTS
    logits = jnp.where(valid, logits, -jnp.inf)
    mx = jnp.max(logits, axis=-1, keepdims=True)
    e = jnp.exp(logits - mx)
    probs = e / jnp.sum(e, axis=-1, keepdims=True)
    p1 = jnp.max(probs, axis=-1, keepdims=True)
    i1 = jnp.min(jnp.where(probs == p1, lane, LANE), axis=-1, keepdims=True)
    rest = jnp.where((lane == i1) | ~valid, -1.0, probs)
    p2 = jnp.max(rest, axis=-1, keepdims=True)
    i2 = jnp.min(jnp.where(rest == p2, lane, LANE), axis=-1, keepdims=True)
    tot = p1 + p2
    gate_ref[...] = jnp.where(lane == i1, p1 / tot, 0.0) + jnp.where(lane == i2, p2 / tot, 0.0)


def rmsnorm_router(x, g, w_router, tm=256):
    m, d = x.shape
    wr = jnp.zeros((d, LANE), F32).at[:, :N_EXPERTS].set(w_router)
    return pl.pallas_call(
        _rmsnorm_router_body,
        grid=(m // tm,),
        in_specs=[pl.BlockSpec((tm, d), lambda i: (i, 0)), pl.BlockSpec((1, d), lambda i: (0, 0)),
                  pl.BlockSpec((d, LANE), lambda i: (0, 0))],
        out_specs=[pl.BlockSpec((tm, d), lambda i: (i, 0)), pl.BlockSpec((tm, LANE), lambda i: (i, 0))],
        out_shape=[jax.ShapeDtypeStruct((m, d), BF16), jax.ShapeDtypeStruct((m, LANE), F32)],
        compiler_params=_cparams(("arbitrary",)),
        name="rmsnorm_router",
    )(x, g.reshape(1, d), wr)


def _mm_body(*refs, nk, has_res):
    if has_res:
        x_ref, w_ref, r_ref, o_ref = refs[:4]
    else:
        x_ref, w_ref, o_ref = refs[:3]
    part = jnp.dot(x_ref[...], w_ref[...], preferred_element_type=F32)
    if nk == 1:
        if has_res:
            part = part + r_ref[...]
        o_ref[...] = part.astype(o_ref.dtype)
        return
    acc_ref = refs[-1]
    k = pl.program_id(2)

    @pl.when(k == 0)
    def _():
        acc_ref[...] = part

    @pl.when(k > 0)
    def _():
        acc_ref[...] += part

    @pl.when(k == nk - 1)
    def _():
        r = acc_ref[...]
        if has_res:
            r = r + r_ref[...]
        o_ref[...] = r.astype(o_ref.dtype)


def matmul(x, w, residual=None, out_dtype=F32, tm=1024, tn=1024, tk=None):
    m, kd = x.shape
    n = w.shape[1]
    tm = min(tm, m)
    tn = min(tn, n)
    tk = kd if tk is None else tk
    nk = kd // tk
    has_res = residual is not None
    in_specs = [pl.BlockSpec((tm, tk), lambda j, i, k: (i, k)), pl.BlockSpec((tk, tn), lambda j, i, k: (k, j))]
    args = [x, w]
    if has_res:
        in_specs.append(pl.BlockSpec((tm, tn), lambda j, i, k: (i, j)))
        args.append(residual)
    return pl.pallas_call(
        functools.partial(_mm_body, nk=nk, has_res=has_res),
        grid=(n // tn, m // tm, nk),
        in_specs=in_specs,
        out_specs=pl.BlockSpec((tm, tn), lambda j, i, k: (i, j)),
        out_shape=jax.ShapeDtypeStruct((m, n), out_dtype),
        scratch_shapes=[pltpu.VMEM((tm, tn), F32)] if nk > 1 else [],
        compiler_params=_cparams(("arbitrary", "arbitrary", "arbitrary")),
        name="matmul",
    )(*args)


def _swiglu_body(*refs, has_gate):
    if has_gate:
        x_ref, wg_ref, wu_ref, gate_ref, o_ref = refs
    else:
        x_ref, wg_ref, wu_ref, o_ref = refs
    x = x_ref[...]
    g = jnp.dot(x, wg_ref[...], preferred_element_type=F32)
    u = jnp.dot(x, wu_ref[...], preferred_element_type=F32)
    a = g * jax.nn.sigmoid(g) * u
    if has_gate:
        e = pl.program_id(0)
        gates = gate_ref[...]
        lane = lax.broadcasted_iota(jnp.int32, gates.shape, 1)
        a = a * jnp.sum(jnp.where(lane == e, gates, 0.0), axis=-1, keepdims=True)
    o_ref[...] = a.astype(o_ref.dtype)


def swiglu_up(x, wg, wu, gates=None, tm=512, tf=1408):
    m, d = x.shape
    ne, _, f = wg.shape
    tm = min(tm, m)
    nf = f // tf
    has_gate = gates is not None
    in_specs = [pl.BlockSpec((tm, d), lambda e, j, i: (i, 0)),
                pl.BlockSpec((None, d, tf), lambda e, j, i: (e, 0, j)),
                pl.BlockSpec((None, d, tf), lambda e, j, i: (e, 0, j))]
    args = [x, wg, wu]
    if has_gate:
        in_specs.append(pl.BlockSpec((tm, LANE), lambda e, j, i: (i, 0)))
        args.append(gates)
    return pl.pallas_call(
        functools.partial(_swiglu_body, has_gate=has_gate),
        grid=(ne, nf, m // tm),
        in_specs=in_specs,
        out_specs=pl.BlockSpec((tm, tf), lambda e, j, i: (i, e * nf + j)),
        out_shape=jax.ShapeDtypeStruct((m, ne * f), BF16),
        compiler_params=_cparams(("arbitrary", "arbitrary", "arbitrary")),
        name="swiglu_up",
    )(*args)


def _rotary_body(x_ref, c_ref, s_ref, o_ref):
    x = x_ref[...]
    n = x.shape[1]
    lane = lax.broadcasted_iota(jnp.int32, x.shape, 1)
    first_half = (lane % RET_DK) < (RET_DK // 2)
    partner = jnp.where(first_half, pltpu.roll(x, n - RET_DK // 2, 1), pltpu.roll(x, RET_DK // 2, 1))
    o_ref[...] = x * c_ref[...] + partner * s_ref[...]


def rotary_qk(p, cos_t, sin_t, tm=512):
    m = p.shape[0]
    n = 2 * RET_QK
    return pl.pallas_call(
        _rotary_body,
        grid=(m // tm,),
        in_specs=[pl.BlockSpec((tm, n), lambda i: (i, 0)), pl.BlockSpec((tm, n), lambda i: (i, 0)),
                  pl.BlockSpec((tm, n), lambda i: (i, 0))],
        out_specs=pl.BlockSpec((tm, n), lambda i: (i, 0)),
        out_shape=jax.ShapeDtypeStruct((m, n), F32),
        compiler_params=_cparams(("arbitrary",)),
        name="rotary_qk",
    )(p, cos_t, sin_t)


def rotary_tables(pos):
    half = RET_DK // 2
    inv = ROPE_BASE ** (-jnp.arange(half, dtype=F32) / half)
    ang = pos[:, None] * inv[None, :]
    c, s = jnp.cos(ang), jnp.sin(ang)
    c_head = jnp.concatenate([c, c], axis=1)
    s_head = jnp.concatenate([-s, s], axis=1)
    c_q = jnp.tile(c_head, (1, RET_HEADS))
    s_q = jnp.tile(s_head, (1, RET_HEADS))
    k_scale = RET_DK ** -0.5
    return jnp.concatenate([c_q, c_q * k_scale], axis=1), jnp.concatenate([s_q, s_q * k_scale], axis=1)


def _ret_prompt_body(qk_ref, v_ref, dmat_ref, rowdec_ref, coldec_ref, sdec_ref, o_ref, s_ref):
    c = pl.program_id(0)

    @pl.when(c == 0)
    def _():
        s_ref[...] = jnp.zeros_like(s_ref)

    qk = qk_ref[...]
    q_all = qk[:, :RET_QK].astype(BF16)
    k_all = qk[:, RET_QK:]
    kdec_all = (k_all * coldec_ref[...]).astype(BF16)
    k_all = k_all.astype(BF16)
    v_all = v_ref[...].astype(BF16)
    for h in range(RET_HEADS):
        q = q_all[:, h * RET_DK:(h + 1) * RET_DK]
        k = k_all[:, h * RET_DK:(h + 1) * RET_DK]
        kdec = kdec_all[:, h * RET_DK:(h + 1) * RET_DK]
        v = v_all[:, h * RET_DV:(h + 1) * RET_DV]
        s = s_ref[h]
        inner = lax.dot_general(q, k, (((1,), (1,)), ((), ())), preferred_element_type=F32) * dmat_ref[h]
        o = jnp.dot(inner.astype(BF16), v, preferred_element_type=F32)
        o = o + jnp.dot(q, s.astype(BF16), preferred_element_type=F32) * rowdec_ref[h]
        o_ref[:, h * RET_DV:(h + 1) * RET_DV] = o
        s_ref[h] = sdec_ref[h] * s + lax.dot_general(kdec, v, (((0,), (0,)), ((), ())),
                                                     preferred_element_type=F32)


def _log_decay():
    return jnp.log1p(-jnp.exp2(-5.0 - jnp.arange(RET_HEADS, dtype=F32)))


def retention_prompt(qk, p):
    t = qk.shape[0]
    cs = RET_CHUNK
    lg = _log_decay()
    idx = jnp.arange(cs, dtype=F32)
    diff = idx[:, None] - idx[None, :]
    dmat = jnp.where(diff >= 0, jnp.exp(lg[:, None, None] * jnp.maximum(diff, 0.0)), 0.0)
    rowdec = jnp.broadcast_to(jnp.exp(lg[:, None] * (idx[None, :] + 1.0))[:, :, None], (RET_HEADS, cs, RET_DV))
    coldec = jnp.repeat(jnp.exp(lg[None, :] * (cs - 1.0 - idx[:, None])), RET_DK, axis=1)
    sdec = jnp.broadcast_to(jnp.exp(lg * cs)[:, None, None], (RET_HEADS, RET_DK, RET_DV))
    const3 = lambda shape: pl.BlockSpec(shape, lambda c: (0, 0, 0))
    return pl.pallas_call(
        _ret_prompt_body,
        grid=(t // cs,),
        in_specs=[pl.BlockSpec((cs, 2 * RET_QK), lambda c: (c, 0)),
                  pl.BlockSpec((cs, RET_WIDTH), lambda c: (c, OFF_RV // RET_WIDTH)),
                  const3((RET_HEADS, cs, cs)), const3((RET_HEADS, cs, RET_DV)),
                  pl.BlockSpec((cs, RET_QK), lambda c: (0, 0)), const3((RET_HEADS, RET_DK, RET_DV))],
        out_specs=[pl.BlockSpec((cs, RET_WIDTH), lambda c: (c, 0)), const3((RET_HEADS, RET_DK, RET_DV))],
        out_shape=[jax.ShapeDtypeStruct((t, RET_WIDTH), F32),
                   jax.ShapeDtypeStruct((RET_HEADS, RET_DK, RET_DV), F32)],
        compiler_params=_cparams(("arbitrary",)),
        name="retention_prompt",
    )(qk, p, dmat, rowdec, coldec, sdec)


def _ret_sample_body(qbd_ref, kbdt_ref, kdecbdt_ref, v_ref, s_ref, m_ref, rowdec_ref, sdec_ref, o_ref, sn_ref, *, bb):
    for i in range(bb):
        qbd = qbd_ref[i]
        v = v_ref[i]
        s = s_ref[i]
        inner = jnp.dot(qbd, kbdt_ref[i], preferred_element_type=F32) * m_ref[...]
        o = jnp.dot(inner.astype(BF16), v, preferred_element_type=F32)
        o = o + jnp.dot(qbd, s.astype(BF16), preferred_element_type=F32) * rowdec_ref[...]
        o_ref[i] = o
        sn_ref[i] = sdec_ref[...] * s + jnp.dot(kdecbdt_ref[i], v, preferred_element_type=F32)


def retention_sample(layer, qk, rv, state, bb=4):
    b = state.shape[1]
    nt, nh = DEC_SEQ, RET_HEADS
    lg = _log_decay()
    q = qk[:, :RET_QK].reshape(b, nt, nh, RET_DK)
    k = qk[:, RET_QK:].reshape(b, nt, nh, RET_DK)
    idx = jnp.arange(nt, dtype=F32)
    kdec = k * jnp.exp(lg[None, :] * (nt - 1.0 - idx[:, None]))[None, :, :, None]
    eye = jnp.eye(nh, dtype=F32)
    qbd = jnp.einsum('bthd,hg->bhtgd', q, eye).reshape(b, nh * nt, nh * RET_DK).astype(BF16)
    kbdt = jnp.einsum('bthd,hg->bhdgt', k, eye).reshape(b, nh * RET_DK, nh * nt).astype(BF16)
    kdecbdt = jnp.einsum('bthd,hg->bhdgt', kdec, eye).reshape(b, nh * RET_DK, nh * nt).astype(BF16)
    v = rv.reshape(b, nt, nh, RET_DV).transpose(0, 2, 1, 3).reshape(b, nh * nt, RET_DV).astype(BF16)
    s = state.reshape(state.shape[0], b, nh * RET_DK, RET_DV)
    diff = idx[:, None] - idx[None, :]
    dmat = jnp.where(diff >= 0, jnp.exp(lg[:, None, None] * jnp.maximum(diff, 0.0)), 0.0)
    mask = jnp.einsum('htu,hg->htgu', dmat, eye).reshape(nh * nt, nh * nt)
    rowdec = jnp.broadcast_to(jnp.exp(lg[:, None] * (idx[None, :] + 1.0)).reshape(nh * nt, 1), (nh * nt, RET_DV))
    sdec = jnp.broadcast_to(jnp.repeat(jnp.exp(lg * nt), RET_DK)[:, None], (nh * RET_DK, RET_DV))
    blk = lambda r, c: pl.BlockSpec((bb, r, c), lambda i: (i, 0, 0))
    cst = lambda r, c: pl.BlockSpec((r, c), lambda i: (0, 0))
    o, s_new = pl.pallas_call(
        functools.partial(_ret_sample_body, bb=bb),
        grid=(b // bb,),
        in_specs=[blk(nh * nt, nh * RET_DK), blk(nh * RET_DK, nh * nt), blk(nh * RET_DK, nh * nt),
                  blk(nh * nt, RET_DV),
                  pl.BlockSpec((None, bb, nh * RET_DK, RET_DV), lambda i: (layer, i, 0, 0)),
                  cst(nh * nt, nh * nt), cst(nh * nt, RET_DV), cst(nh * RET_DK, RET_DV)],
        out_specs=[blk(nh * nt, RET_DV), blk(nh * RET_DK, RET_DV)],
        out_shape=[jax.ShapeDtypeStruct((b, nh * nt, RET_DV), F32),
                   jax.ShapeDtypeStruct((b, nh * RET_DK, RET_DV), F32)],
        compiler_params=_cparams(("arbitrary",)),
        name="retention_sample",
    )(qbd, kbdt, kdecbdt, v, s, mask, rowdec, sdec)
    o = o.reshape(b, nh, nt, RET_DV).transpose(0, 2, 1, 3).reshape(b * nt, RET_WIDTH)
    return o, s_new.reshape(b, nh, RET_DK, RET_DV)


def _ret_post_body(o_ref, g_ref, gain_ref, out_ref):
    gate = g_ref[...]
    gate = gate * jax.nn.sigmoid(gate)
    for h in range(RET_HEADS):
        sl = slice(h * RET_DV, (h + 1) * RET_DV)
        o = o_ref[:, sl]
        ro = o * lax.rsqrt(jnp.mean(o * o, axis=-1, keepdims=True) + EPS)
        out_ref[:, sl] = (gate[:, sl] * (ro * gain_ref[:, sl])).astype(out_ref.dtype)


def retention_post(o, p, gain, tm=512):
    m = o.shape[0]
    tm = min(tm, m)
    return pl.pallas_call(
        _ret_post_body,
        grid=(m // tm,),
        in_specs=[pl.BlockSpec((tm, RET_WIDTH), lambda i: (i, 0)),
                  pl.BlockSpec((tm, RET_WIDTH), lambda i: (i, OFF_RG // RET_WIDTH)),
                  pl.BlockSpec((1, RET_WIDTH), lambda i: (0, 0))],
        out_specs=pl.BlockSpec((tm, RET_WIDTH), lambda i: (i, 0)),
        out_shape=jax.ShapeDtypeStruct((m, RET_WIDTH), BF16),
        compiler_params=_cparams(("arbitrary",)),
        name="retention_post",
    )(o, p, gain.reshape(1, RET_WIDTH))


def _cumsum_matrix():
    kp = jnp.arange(2 * SB_TK)[:, None] % SB_TK
    col = jnp.arange(2 * SB_TK)[None, :]
    return jnp.where(jnp.where(col < SB_TK, kp > col, True), -1.0, 0.0).astype(BF16)


def _sb_scores(z, uo, carry, masks):
    g = z.shape[1] // SB_TK
    sp = jnp.maximum(z, 0.0) + jnp.log2(1.0 + jnp.exp2(-jnp.abs(z)))
    logbeta = z - sp
    w = [None] * g
    for s in reversed(range(g)):
        sl = slice(s * SB_TK, (s + 1) * SB_TK)
        nlp = sp[:, sl]
        if masks is not None:
            nlp = jnp.where(masks[s], nlp, 0.0)
        hi = nlp.astype(BF16)
        lo = (nlp - hi.astype(F32)).astype(BF16)
        cs = jnp.dot(jnp.concatenate([hi, lo], axis=1), uo, preferred_element_type=F32)
        ws = jnp.exp2(logbeta[:, sl] + (carry + cs[:, :SB_TK]))
        if masks is not None:
            ws = jnp.where(masks[s], ws, 0.0)
        w[s] = ws.astype(BF16)
        carry = carry + cs[:, SB_TK:]
    return (w[0] if g == 1 else jnp.concatenate(w, axis=1)), carry


def _sb_prompt_body(bias_ref, q_ref, k_ref, v_ref, uo_ref, o_ref, kb_ref, vb_ref, *, tq):
    h = pl.program_id(0)
    i = pl.program_id(1)

    @pl.when(i == 0)
    def _():
        kb_ref[...] = k_ref[...].astype(BF16)
        vb_ref[...] = v_ref[...].astype(BF16)

    bias = bias_ref[h] * LOG2E
    q = (q_ref[...] * SB_QSCALE).astype(BF16)
    uo = uo_ref[...]
    nd = tq // SB_TK

    def block(start, carry, acc, masks):
        kw = kb_ref[pl.ds(start, tq), :]
        z = lax.dot_general(q, kw, (((1,), (1,)), ((), ())), preferred_element_type=F32) + bias
        w, carry = _sb_scores(z, uo, carry, masks)
        return carry, acc + jnp.dot(w, vb_ref[pl.ds(start, tq), :], preferred_element_type=F32)

    row = lax.broadcasted_iota(jnp.int32, (tq, SB_TK), 0)
    col = lax.broadcasted_iota(jnp.int32, (tq, SB_TK), 1)
    masks = [(col + s * SB_TK) < row for s in range(nd)]
    carry, acc = block(pl.multiple_of(i * tq, tq), jnp.zeros((tq, SB_TK), F32), jnp.zeros((tq, SB_DH), F32), masks)

    def body(t, ca):
        return block(pl.multiple_of((i - 1 - t) * tq, tq), ca[0], ca[1], None)

    carry, acc = lax.fori_loop(0, i, body, (carry, acc))
    o_ref[...] = acc.astype(o_ref.dtype)


def sb_prompt(p, bias, tq=512):
    t = p.shape[0]
    cq, ck, cv = OFF_SQ // SB_DH, OFF_SK // SB_DH, OFF_SV // SB_DH
    grid_spec = pltpu.PrefetchScalarGridSpec(
        num_scalar_prefetch=1,
        grid=(SB_HEADS, t // tq),
        in_specs=[pl.BlockSpec((tq, SB_DH), lambda h, i, b: (i, cq + h)),
                  pl.BlockSpec((t, SB_DH), lambda h, i, b: (0, ck + h)),
                  pl.BlockSpec((t, SB_DH), lambda h, i, b: (0, cv + h)),
                  pl.BlockSpec((2 * SB_TK, 2 * SB_TK), lambda h, i, b: (0, 0))],
        out_specs=pl.BlockSpec((tq, SB_DH), lambda h, i, b: (i, h)),
        scratch_shapes=[pltpu.VMEM((t, SB_DH), BF16), pltpu.VMEM((t, SB_DH), BF16)],
    )
    return pl.pallas_call(
        functools.partial(_sb_prompt_body, tq=tq),
        grid_spec=grid_spec,
        out_shape=jax.ShapeDtypeStruct((t, SB_WIDTH), BF16),
        compiler_params=_cparams(("arbitrary", "arbitrary")),
        name="sb_prompt",
    )(bias, p, p, p, _cumsum_matrix())


def _sb_sample_body(pt_ref, q_ref, kn_ref, vn_ref, *refs, n_steps, pps):
    k_refs, v_refs = refs[:pps], refs[pps:2 * pps]
    uo_ref, bias_ref, o_ref, kpad_ref, vpad_ref, carry_ref, acc_ref = refs[2 * pps:]
    b = pl.program_id(0)
    j = pl.program_id(1)
    nh = SB_HEADS
    rows = nh * ROW_PAD
    new_rows = DEC_SEQ * nh

    @pl.when((b == 0) & (j == 0))
    def _():
        kpad_ref[...] = jnp.zeros_like(kpad_ref)
        vpad_ref[...] = jnp.zeros_like(vpad_ref)

    def head(src, h):
        return src[pl.ds(h, PAGE_SIZE, stride=nh), :].astype(BF16)

    def page(ksrc, vsrc, carry, acc, mask):
        z = jnp.concatenate(
            [lax.dot_general(q_ref[h], head(ksrc, h), (((1,), (1,)), ((), ())), preferred_element_type=F32)
             for h in range(nh)], axis=0) + bias_ref[...]
        w, carry = _sb_scores(z, uo_ref[...], carry, None if mask is None else [mask])
        acc = [acc[h] + jnp.dot(w[h * ROW_PAD:(h + 1) * ROW_PAD], head(vsrc, h), preferred_element_type=F32)
               for h in range(nh)]
        return carry, acc

    def pages(carry, acc):
        for u in range(pps):
            carry, acc = page(k_refs[u], v_refs[u], carry, acc, None)
        carry_ref[...] = carry
        for h in range(nh):
            acc_ref[h] = acc[h]

    @pl.when(j == 0)
    def _():
        kpad_ref[0:new_rows] = kn_ref[...]
        vpad_ref[0:new_rows] = vn_ref[...]
        row = lax.broadcasted_iota(jnp.int32, (rows, SB_TK), 0)
        col = lax.broadcasted_iota(jnp.int32, (rows, SB_TK), 1)
        carry, acc = page(kpad_ref, vpad_ref, jnp.zeros((rows, SB_TK), F32),
                          [jnp.zeros((ROW_PAD, SB_DH), F32)] * nh, col < (row % ROW_PAD))
        pages(carry, acc)

    @pl.when(j > 0)
    def _():
        pages(carry_ref[...], [acc_ref[h] for h in range(nh)])

    @pl.when(j == n_steps - 1)
    def _():
        o_ref[...] = acc_ref[...]


def sb_sample(layer, sq, sk, sv, cache_k, cache_v, page_table, bias, pps=4):
    b, n_pages = page_table.shape
    nt, nh, dh = DEC_SEQ, SB_HEADS, SB_DH
    n_steps = n_pages // pps
    q = sq.reshape(b, nt, nh, dh).transpose(0, 2, 1, 3) * SB_QSCALE
    q = jnp.pad(q, ((0, 0), (0, 0), (0, ROW_PAD - nt), (0, 0))).astype(BF16)
    kn = sk.reshape(b, nt * nh, dh)
    vn = sv.reshape(b, nt * nh, dh)
    bias_rows = jnp.broadcast_to(jnp.repeat(bias.astype(F32) * LOG2E, ROW_PAD)[:, None], (nh * ROW_PAD, SB_TK))

    def page_spec(u):
        return pl.BlockSpec((None, None, PAGE_SIZE * nh, dh),
                            lambda bi, j, pt: (layer, pt[bi * n_pages + n_pages - 1 - (j * pps + u)], 0, 0))

    per_seq = lambda r, c: pl.BlockSpec((None, r, c), lambda bi, j, pt: (bi, 0, 0))
    grid_spec = pltpu.PrefetchScalarGridSpec(
        num_scalar_prefetch=1,
        grid=(b, n_steps),
        in_specs=[pl.BlockSpec((None, nh, ROW_PAD, dh), lambda bi, j, pt: (bi, 0, 0, 0)),
                  per_seq(nt * nh, dh), per_seq(nt * nh, dh)]
                 + [page_spec(u) for u in range(pps)] * 2
                 + [pl.BlockSpec((2 * SB_TK, 2 * SB_TK), lambda bi, j, pt: (0, 0)),
                    pl.BlockSpec((nh * ROW_PAD, SB_TK), lambda bi, j, pt: (0, 0))],
        out_specs=pl.BlockSpec((None, nh, ROW_PAD, dh), lambda bi, j, pt: (bi, 0, 0, 0)),
        scratch_shapes=[pltpu.VMEM((PAGE_SIZE * nh, dh), F32), pltpu.VMEM((PAGE_SIZE * nh, dh), F32),
                        pltpu.VMEM((nh * ROW_PAD, SB_TK), F32), pltpu.VMEM((nh, ROW_PAD, dh), F32)],
    )
    o = pl.pallas_call(
        functools.partial(_sb_sample_body, n_steps=n_steps, pps=pps),
        grid_spec=grid_spec,
        out_shape=jax.ShapeDtypeStruct((b, nh, ROW_PAD, dh), F32),
        compiler_params=_cparams(("arbitrary", "arbitrary")),
        name="sb_sample",
    )(page_table.reshape(-1), q, kn, vn, *([cache_k] * pps), *([cache_v] * pps), _cumsum_matrix(), bias_rows)
    return o[:, :, :nt, :].transpose(0, 2, 1, 3).reshape(b * nt, nh * dh)


def _mixer(x, l, is_prompt, wts, cos_t, sin_t, cache_k, cache_v, state_l, page_table):
    h = rmsnorm(x, wts['g_mix'][l], BF16)
    p = matmul(h, wts['w_in'][l])
    qk = rotary_qk(p, cos_t, sin_t)
    sk = p[:, OFF_SK:OFF_SV]
    sv = p[:, OFF_SV:]
    if is_prompt:
        ret_o, s_new = retention_prompt(qk, p)
        s_new = s_new[None]
        sb_o = sb_prompt(p, wts['sb_bias'][l])
    else:
        ret_o, s_new = retention_sample(l, qk, p[:, OFF_RV:OFF_RG], state_l)
        sb_o = sb_sample(l, p[:, OFF_SQ:OFF_SK], sk, sv, cache_k, cache_v, page_table,
                         wts['sb_bias'][l]).astype(BF16)
    ro = retention_post(ret_o, p, wts['ret_norm_g'][l])
    cat = jnp.concatenate([ro, sb_o], axis=1)
    x = matmul(cat, wts['w_out'][l], residual=x)
    return x, sk, sv, s_new


def _ffn(x, l, wts):
    i = l // 2
    if l % 2 == 0:
        h = rmsnorm(x, wts['g_ffn'][l], BF16)
        a = swiglu_up(h, wts['w_ff_gate'][i:i + 1], wts['w_ff_up'][i:i + 1])
        return matmul(a, wts['w_ff_down'][i], residual=x, tk=1408)
    h, gates = rmsnorm_router(x, wts['g_ffn'][l], wts['w_router'][i])
    a = swiglu_up(h, wts['w_exp_gate'][i], wts['w_exp_up'][i], gates=gates)
    wd = wts['w_exp_down'][i]
    return matmul(a, wd.reshape(wd.shape[0] * wd.shape[1], wd.shape[2]), residual=x, tk=1408)


def kernel(x_prompt, x_sample, cache_sb_k, cache_sb_v, state_ret, page_table, g_mix, w_in, ret_norm_g, sb_bias,
           w_out, g_ffn, w_ff_gate, w_ff_up, w_ff_down, w_router, w_exp_gate, w_exp_up, w_exp_down, g_final):
    wts = dict(g_mix=g_mix, w_in=w_in.astype(BF16), ret_norm_g=ret_norm_g, sb_bias=sb_bias,
               w_out=w_out.astype(BF16), g_ffn=g_ffn, w_ff_gate=w_ff_gate.astype(BF16),
               w_ff_up=w_ff_up.astype(BF16), w_ff_down=w_ff_down.astype(BF16), w_router=w_router,
               w_exp_gate=w_exp_gate.astype(BF16), w_exp_up=w_exp_up.astype(BF16),
               w_exp_down=w_exp_down.astype(BF16))
    n_s = DEC_BATCH * DEC_SEQ
    xp = x_prompt.reshape(SEQ, D_MODEL)
    xs = x_sample.reshape(n_s, D_MODEL)
    cos_p, sin_p = rotary_tables(jnp.arange(SEQ, dtype=F32))
    cos_s, sin_s = rotary_tables(jnp.tile(PAST_LEN + jnp.arange(DEC_SEQ, dtype=F32), DEC_BATCH))
    cache_k = cache_sb_k.reshape(DEPTH, -1, PAGE_SIZE * SB_HEADS, SB_DH)
    cache_v = cache_sb_v.reshape(DEPTH, -1, PAGE_SIZE * SB_HEADS, SB_DH)
    outs = [[] for _ in range(6)]
    for l in range(DEPTH):
        xp, pk, pv, ps = _mixer(xp, l, True, wts, cos_p, sin_p, None, None, None, None)
        xs, sk, sv, ss = _mixer(xs, l, False, wts, cos_s, sin_s, cache_k, cache_v, state_ret, page_table)
        for lst, val in zip(outs, (pk.reshape(1, SEQ, SB_HEADS, SB_DH), pv.reshape(1, SEQ, SB_HEADS, SB_DH), ps,
                                   sk.reshape(DEC_BATCH, DEC_SEQ, SB_HEADS, SB_DH),
                                   sv.reshape(DEC_BATCH, DEC_SEQ, SB_HEADS, SB_DH), ss)):
            lst.append(val)
        xp = _ffn(xp, l, wts)
        xs = _ffn(xs, l, wts)
    y_prompt = rmsnorm(xp, g_final, F32).reshape(1, SEQ, D_MODEL)
    y_sample = rmsnorm(xs, g_final, F32).reshape(DEC_BATCH, DEC_SEQ, D_MODEL)
    return (y_prompt, y_sample) + tuple(jnp.stack(lst) for lst in outs)
```
